```python
import jax, jax.numpy as jnp
from jax import lax
import numpy as np

D_MODEL = 1024
BATCH = 4
SEQ = 8192
DEPTH = 2

N_MIXERS = 2
ATTN_HEADS = 8
HEAD_DIM = D_MODEL // ATTN_HEADS
MOBA_BLOCK = 256
MOBA_TOPK = 3
Q_CHUNK = 16
CONV_WIDTH = 3
FFN_HIDDEN = ((-(-8 * D_MODEL // 3) + 255) // 256) * 256
RMS_EPS = 1e-6
N_ATTN_LAYERS = (DEPTH + N_MIXERS - 1) // N_MIXERS
N_CONV_LAYERS = DEPTH // N_MIXERS

kernel_name = "moba_shortconv_hybrid_trunk"


def rms_norm(x, g):
    xf = x.astype(jnp.float32)
    y = xf * lax.rsqrt(jnp.mean(xf * xf, axis=-1, keepdims=True) + RMS_EPS)
    return (y * g.astype(jnp.float32)).astype(x.dtype)


def moba_attention(h, w_qkv, w_o):
    B, S, _ = h.shape
    H, Dh, BS = ATTN_HEADS, HEAD_DIM, MOBA_BLOCK
    nb = -(-S // BS)
    pad = nb * BS - S
    qkv = jnp.einsum('bsd,de->bse', h, w_qkv).reshape(B, S, 3, H, Dh)
    q = jnp.transpose(qkv[:, :, 0], (0, 2, 1, 3))
    k = jnp.pad(jnp.transpose(qkv[:, :, 1], (0, 2, 1, 3)), ((0, 0), (0, 0), (0, pad), (0, 0)))
    v = jnp.pad(jnp.transpose(qkv[:, :, 2], (0, 2, 1, 3)), ((0, 0), (0, 0), (0, pad), (0, 0)))
    kb = k.reshape(B, H, nb, BS, Dh)
    vb = v.reshape(B, H, nb, BS, Dh)
    k_mean = jnp.mean(kb.astype(jnp.float32), axis=3)
    topk = min(MOBA_TOPK, nb)
    n_chunks = S // Q_CHUNK
    q_chunks = jnp.moveaxis(q.reshape(B, H, n_chunks, Q_CHUNK, Dh), 2, 0)
    starts = jnp.arange(n_chunks, dtype=jnp.int32) * Q_CHUNK
    b_ix = jnp.arange(B)[:, None, None, None]
    h_ix = jnp.arange(H)[None, :, None, None]
    blk_ids = jnp.arange(nb, dtype=jnp.int32)
    offs = jnp.arange(BS, dtype=jnp.int32)
    scale = Dh ** -0.5
    neg = jnp.finfo(jnp.float32).min

    def one_chunk(args):
        qc, start = args
        t = start + jnp.arange(Q_CHUNK, dtype=jnp.int32)
        own = t // BS
        gate = jnp.einsum('bhqd,bhnd->bhqn', qc.astype(jnp.float32), k_mean)
        past = blk_ids[None, :] < own[:, None]
        gate = jnp.where(past, gate, neg)
        _, sel = lax.top_k(gate, topk)
        own_b = jnp.broadcast_to(own[None, None, :, None], (B, H, Q_CHUNK, 1))
        idx = jnp.concatenate([sel, own_b], axis=-1)
        valid_blk = jnp.concatenate(
            [sel < own[None, None, :, None], jnp.ones((B, H, Q_CHUNK, 1), dtype=bool)], axis=-1)
        kpos = idx[..., None] * BS + offs
        mask = valid_blk[..., None] & (kpos <= t[None, None, :, None, None])
        kg = kb[b_ix, h_ix, idx]
        vg = vb[b_ix, h_ix, idx]
        s = jnp.einsum('bhqd,bhqnkd->bhqnk', qc, kg).astype(jnp.float32) * scale
        s = jnp.where(mask, s, neg)
        p = jax.nn.softmax(s.reshape(B, H, Q_CHUNK, -1), axis=-1).reshape(s.shape)
        return jnp.einsum('bhqnk,bhqnkd->bhqd', p.astype(vg.dtype), vg)

    out = lax.map(one_chunk, (q_chunks, starts))
    out = jnp.moveaxis(out, 0, 2).reshape(B, H, S, Dh)
    out = jnp.transpose(out, (0, 2, 1, 3)).reshape(B, S, H * Dh)
    return jnp.einsum('bse,ed->bsd', out, w_o)


def short_conv_mixer(h, w_in, conv_w, w_out):
    S = h.shape[1]
    bcx = jnp.einsum('bsd,de->bse', h, w_in)
    b_gate, c_gate, xt = jnp.split(bcx, 3, axis=-1)
    u = c_gate * xt
    up = jnp.pad(u, ((0, 0), (CONV_WIDTH - 1, 0), (0, 0)))
    conv = conv_w[0] * up[:, 0:S]
    for j in range(1, CONV_WIDTH):
        conv = conv + conv_w[j] * up[:, j:j + S]
    return jnp.einsum('bsd,de->bse', b_gate * conv, w_out)


def swiglu(h, w_in, w_out):
    g, u = jnp.split(jnp.einsum('bsd,df->bsf', h, w_in), 2, axis=-1)
    return jnp.einsum('bsf,fd->bsd', jax.nn.silu(g) * u, w_out)


def setup_inputs(seed: int = 0) -> dict:
    key = jax.random.key(seed)
    ks = jax.random.split(key, 12)
    D, F = D_MODEL, FFN_HIDDEN
    na, nc = N_ATTN_LAYERS, N_CONV_LAYERS
    f32 = jnp.float32

    def w(k, shape, fan_in):
        return jax.random.normal(k, shape, f32) * (fan_in ** -0.5)

    def gain(k, shape):
        return 1.0 + 0.05 * jax.random.normal(k, shape, f32)

    return {
        "x": jax.random.normal(ks[0], (BATCH, SEQ, D), f32),
        "attn_norm": gain(ks[1], (na, D)),
        "attn_w_qkv": w(ks[2], (na, D, 3 * ATTN_HEADS * HEAD_DIM), D),
        "attn_w_o": w(ks[3], (na, ATTN_HEADS * HEAD_DIM, D), ATTN_HEADS * HEAD_DIM),
        "conv_norm": gain(ks[4], (nc, D)),
        "conv_w_in": w(ks[5], (nc, D, 3 * D), D),
        "conv_w": w(ks[6], (nc, CONV_WIDTH, D), CONV_WIDTH),
        "conv_w_out": w(ks[7], (nc, D, D), D),
        "ffn_norm": gain(ks[8], (DEPTH, D)),
        "ffn_w_in": w(ks[9], (DEPTH, D, 2 * F), D),
        "ffn_w_out": w(ks[10], (DEPTH, F, D), F),
        "final_norm": gain(ks[11], (D,)),
    }


def reference(x, attn_norm, attn_w_qkv, attn_w_o, conv_norm, conv_w_in, conv_w, conv_w_out,
              ffn_norm, ffn_w_in, ffn_w_out, final_norm):
    for i in range(DEPTH):
        j = i // N_MIXERS
        if i % N_MIXERS == 0:
            x = x + moba_attention(rms_norm(x, attn_norm[j]), attn_w_qkv[j], attn_w_o[j])
        else:
            x = x + short_conv_mixer(rms_norm(x, conv_norm[j]), conv_w_in[j], conv_w[j], conv_w_out[j])
        x = x + swiglu(rms_norm(x, ffn_norm[i]), ffn_w_in[i], ffn_w_out[i])
    return rms_norm(x, final_norm)
```

```python
import functools
import math

import jax
import jax.numpy as jnp
from jax import lax
from jax.experimental import pallas as pl
from jax.experimental.pallas import tpu as pltpu

ATTN_HEADS = 8
HEAD_DIM = 128
MOBA_BLOCK = 256
MOBA_TOPK = 3
CONV_WIDTH = 3
RMS_EPS = 1e-6
MASK_VALUE = -1e30

QKV_ROWS = 512
FFN_ROWS = 512
FFN_CHUNK = 256
CONV_HALO = 8

V7X_VMEM_BYTES = 64 * 1024 * 1024

_F32 = jnp.float32
_BF16 = jnp.bfloat16
_NT = (((1,), (1,)), ((), ()))


def _vmem_limit(estimate_bytes):
    return int(min(V7X_VMEM_BYTES - (4 << 20), max(32 << 20, estimate_bytes)))


def _rms(x, g):
    var = jnp.mean(x * x, axis=-1, keepdims=True)
    return (x * lax.rsqrt(var + RMS_EPS)) * g


def _resident(shape):
    return pl.BlockSpec(shape, lambda *_: (0,) * len(shape), pipeline_mode=pl.Buffered(1))


def _qkv_kernel(x_ref, g_ref, wqk_ref, wvt_ref, q_ref, k_ref, vt_ref, km_ref, *, q_scale):
    d = x_ref.shape[1]
    rows = x_ref.shape[0]
    h = _rms(x_ref[...], g_ref[...]).astype(_BF16)
    qk = jnp.dot(h, wqk_ref[...], preferred_element_type=_F32)
    for hh in range(ATTN_HEADS):
        lo = hh * HEAD_DIM
        q_ref[hh] = (qk[:, lo:lo + HEAD_DIM] * q_scale).astype(_BF16)
        k_ref[hh] = qk[:, d + lo:d + lo + HEAD_DIM].astype(_BF16)
    kf = qk[:, d:]
    nblk = rows // MOBA_BLOCK
    km_ref[0] = jnp.sum(kf.reshape(nblk, MOBA_BLOCK, d), axis=1) * (1.0 / MOBA_BLOCK)
    vt = lax.dot_general(wvt_ref[...], h, _NT, preferred_element_type=_F32)
    for bi in range(nblk):
        vt_ref[bi] = vt[:, bi * MOBA_BLOCK:(bi + 1) * MOBA_BLOCK].astype(_BF16)


def _qkv_call(x, g, wqk, wvt):
    t, d = x.shape
    rows = QKV_ROWS
    nblk = rows // MOBA_BLOCK
    q_scale = (HEAD_DIM ** -0.5) * math.log2(math.e)
    est = (2 * rows * d * 4 + (2 * d * d + d * d) * 2 + 2 * 3 * rows * d * 2
           + rows * 3 * d * 4 * 2)
    return pl.pallas_call(
        functools.partial(_qkv_kernel, q_scale=q_scale),
        grid=(t // rows,),
        in_specs=[
            pl.BlockSpec((rows, d), lambda i: (i, 0)),
            _resident((1, d)),
            _resident((d, 2 * d)),
            _resident((d, d)),
        ],
        out_specs=[
            pl.BlockSpec((ATTN_HEADS, rows, HEAD_DIM), lambda i: (0, i, 0)),
            pl.BlockSpec((ATTN_HEADS, rows, HEAD_DIM), lambda i: (0, i, 0)),
            pl.BlockSpec((nblk, d, MOBA_BLOCK), lambda i: (i, 0, 0)),
            pl.BlockSpec((1, nblk, d), lambda i: (i, 0, 0)),
        ],
        out_shape=[
            jax.ShapeDtypeStruct((ATTN_HEADS, t, HEAD_DIM), _BF16),
            jax.ShapeDtypeStruct((ATTN_HEADS, t, HEAD_DIM), _BF16),
            jax.ShapeDtypeStruct((t // MOBA_BLOCK, d, MOBA_BLOCK), _BF16),
            jax.ShapeDtypeStruct((t // rows, nblk, d), _F32),
        ],
        compiler_params=pltpu.CompilerParams(
            dimension_semantics=("arbitrary",), vmem_limit_bytes=_vmem_limit(est)),
        name="qkv_proj",
    )(x, g, wqk, wvt)


def _attn_kernel(q_ref, k_ref, vt_ref, km_ref, o_ref, bias_ref):
    nb = km_ref.shape[0]
    bs = MOBA_BLOCK
    km = km_ref[...].astype(_BF16)
    blk = lax.broadcasted_iota(jnp.int32, (nb, bs), 0)
    causal = (lax.broadcasted_iota(jnp.int32, (bs, bs), 0)
              <= lax.broadcasted_iota(jnp.int32, (bs, bs), 1))

    def tile(i, carry):
        r0 = pl.multiple_of(i * bs, bs)
        q = q_ref[0, pl.ds(r0, bs), :]

        gate = lax.dot_general(km, q, _NT, preferred_element_type=_F32)
        cand = blk < i
        chosen = jnp.zeros((nb, bs), dtype=jnp.bool_)
        for _ in range(MOBA_TOPK):
            g = jnp.where(cand, gate, -jnp.inf)
            hit = cand & (g == jnp.max(g, axis=0, keepdims=True))
            first = jnp.min(jnp.where(hit, blk, nb), axis=0, keepdims=True)
            pick = blk == first
            chosen = chosen | pick
            cand = cand & jnp.logical_not(pick)
        bias_ref[...] = jnp.where(chosen, 0.0, MASK_VALUE)

        s = lax.dot_general(k_ref[0, pl.ds(r0, bs), :], q, _NT, preferred_element_type=_F32)
        s = jnp.where(causal, s, MASK_VALUE)
        m = jnp.max(s, axis=0, keepdims=True)
        p = jnp.exp2(s - m)
        l = jnp.sum(p, axis=0, keepdims=True)
        acc = jnp.dot(vt_ref[i], p.astype(_BF16), preferred_element_type=_F32)

        def past(j, state):
            m, l, acc = state
            c0 = pl.multiple_of(j * bs, bs)
            s = lax.dot_general(k_ref[0, pl.ds(c0, bs), :], q, _NT, preferred_element_type=_F32)
            s = s + bias_ref[pl.ds(j, 1), :]
            m_new = jnp.maximum(m, jnp.max(s, axis=0, keepdims=True))
            alpha = jnp.exp2(m - m_new)
            p = jnp.exp2(s - m_new)
            l = alpha * l + jnp.sum(p, axis=0, keepdims=True)
            acc = alpha * acc + jnp.dot(vt_ref[j], p.astype(_BF16), preferred_element_type=_F32)
            return m_new, l, acc

        m, l, acc = lax.fori_loop(0, i, past, (m, l, acc))
        o_ref[pl.ds(r0, bs), :] = jnp.transpose(acc / l).astype(o_ref.dtype)
        return carry

    lax.fori_loop(0, nb, tile, 0)


def _attn_call(q, k, vt, km, batch, seq):
    nb = seq // MOBA_BLOCK
    t = batch * seq
    d = ATTN_HEADS * HEAD_DIM
    est = 2 * 4 * seq * HEAD_DIM * 2 + (8 << 20)
    return pl.pallas_call(
        _attn_kernel,
        grid=(batch, ATTN_HEADS),
        in_specs=[
            pl.BlockSpec((1, seq, HEAD_DIM), lambda b, h: (h, b, 0)),
            pl.BlockSpec((1, seq, HEAD_DIM), lambda b, h: (h, b, 0)),
            pl.BlockSpec((nb, HEAD_DIM, MOBA_BLOCK), lambda b, h: (b, h, 0)),
            pl.BlockSpec((nb, HEAD_DIM), lambda b, h: (b, h)),
        ],
        out_specs=pl.BlockSpec((seq, HEAD_DIM), lambda b, h: (b, h)),
        out_shape=jax.ShapeDtypeStruct((t, d), _BF16),
        scratch_shapes=[pltpu.VMEM((nb, MOBA_BLOCK), _F32)],
        compiler_params=pltpu.CompilerParams(
            dimension_semantics=("arbitrary", "arbitrary"), vmem_limit_bytes=_vmem_limit(est)),
        name="moba_attn",
    )(q, k, vt, km)


def _ffn_kernel(x_ref, a_ref, wp_ref, g_ref, win_ref, wout_ref, gf_ref, o_ref, *, final_norm):
    f = wout_ref.shape[0]
    x1 = x_ref[...] + jnp.dot(a_ref[...], wp_ref[...], preferred_element_type=_F32)
    h = _rms(x1, g_ref[...]).astype(_BF16)
    acc = x1
    for c in range(f // FFN_CHUNK):
        lo = c * FFN_CHUNK
        gate = jnp.dot(h, win_ref[:, lo:lo + FFN_CHUNK], preferred_element_type=_F32)
        up = jnp.dot(h, win_ref[:, f + lo:f + lo + FFN_CHUNK], preferred_element_type=_F32)
        act = (gate * jax.nn.sigmoid(gate) * up).astype(_BF16)
        acc = acc + jnp.dot(act, wout_ref[lo:lo + FFN_CHUNK, :], preferred_element_type=_F32)
    if final_norm:
        acc = _rms(acc, gf_ref[...])
    o_ref[...] = acc


def _ffn_call(x, a, wp, g, win, wout, gf, final_norm):
    t, d = x.shape
    f = wout.shape[0]
    rows = FFN_ROWS
    est = ((d * d + 3 * d * f) * 2 + 2 * rows * d * (4 + 2 + 4)
           + rows * d * 4 * 2 + rows * FFN_CHUNK * 4 * 4 + (4 << 20))
    return pl.pallas_call(
        functools.partial(_ffn_kernel, final_norm=final_norm),
        grid=(t // rows,),
        in_specs=[
            pl.BlockSpec((rows, d), lambda i: (i, 0)),
            pl.BlockSpec((rows, d), lambda i: (i, 0)),
            _resident((d, d)),
            _resident((1, d)),
            _resident((d, 2 * f)),
            _resident((f, d)),
            _resident((1, d)),
        ],
        out_specs=pl.BlockSpec((rows, d), lambda i: (i, 0)),
        out_shape=jax.ShapeDtypeStruct((t, d), _F32),
        compiler_params=pltpu.CompilerParams(
            dimension_semantics=("arbitrary",), vmem_limit_bytes=_vmem_limit(est)),
        name="proj_swiglu_final" if final_norm else "proj_swiglu",
    )(x, a, wp, g, win, wout, gf)


def _conv_kernel(x_ref, g_ref, win_ref, cw_ref, o_ref, ubuf, *, tiles_per_seq):
    rows, d = x_ref.shape

    @pl.when(pl.program_id(0) % tiles_per_seq == 0)
    def _():
        ubuf[0:CONV_HALO, :] = jnp.zeros((CONV_HALO, d), _F32)

    h = _rms(x_ref[...], g_ref[...]).astype(_BF16)
    bcx = jnp.dot(h, win_ref[...], preferred_element_type=_F32)
    u = bcx[:, d:2 * d] * bcx[:, 2 * d:]
    ubuf[CONV_HALO:CONV_HALO + rows, :] = u
    conv = cw_ref[0:1, :] * ubuf[CONV_HALO - 2:CONV_HALO - 2 + rows, :]
    conv = conv + cw_ref[1:2, :] * ubuf[CONV_HALO - 1:CONV_HALO - 1 + rows, :]
    conv = conv + cw_ref[2:3, :] * u
    o_ref[...] = (bcx[:, :d] * conv).astype(o_ref.dtype)
    ubuf[0:CONV_HALO, :] = ubuf[rows:rows + CONV_HALO, :]


def _conv_call(x, g, win, cw, seq):
    t, d = x.shape
    rows = QKV_ROWS
    est = 2 * rows * d * 4 + 3 * d * d * 2 + 2 * rows * d * 2 + rows * 3 * d * 4 * 2 + (4 << 20)
    return pl.pallas_call(
        functools.partial(_conv_kernel, tiles_per_seq=seq // rows),
        grid=(t // rows,),
        in_specs=[
            pl.BlockSpec((rows, d), lambda i: (i, 0)),
            _resident((1, d)),
            _resident((d, 3 * d)),
            _resident((CONV_WIDTH, d)),
        ],
        out_specs=pl.BlockSpec((rows, d), lambda i: (i, 0)),
        out_shape=jax.ShapeDtypeStruct((t, d), _BF16),
        scratch_shapes=[pltpu.VMEM((rows + CONV_HALO, d), _F32)],
        compiler_params=pltpu.CompilerParams(
            dimension_semantics=("arbitrary",), vmem_limit_bytes=_vmem_limit(est)),
        name="shortconv_front",
    )(x, g, win, cw)


def kernel(x, attn_norm, attn_w_qkv, attn_w_o, conv_norm, conv_w_in, conv_w, conv_w_out,
           ffn_norm, ffn_w_in, ffn_w_out, final_norm):
    batch, seq, d = x.shape
    assert d == ATTN_HEADS * HEAD_DIM
    assert seq % QKV_ROWS == 0 and seq % FFN_ROWS == 0 and QKV_ROWS % MOBA_BLOCK == 0
    assert (seq // MOBA_BLOCK) % 8 == 0 and seq // MOBA_BLOCK >= MOBA_TOPK
    assert attn_w_qkv.shape[0] == 1 and conv_w_in.shape[0] == 1 and ffn_w_in.shape[0] == 2
    assert conv_w.shape[1] == CONV_WIDTH and ffn_w_out.shape[1] % FFN_CHUNK == 0
    t = batch * seq
    xf = x.reshape(t, d)

    wqkv = attn_w_qkv[0]
    wqk = wqkv[:, :2 * d].astype(_BF16)
    wvt = wqkv[:, 2 * d:].T.astype(_BF16)
    q, k, vt, km = _qkv_call(xf, attn_norm[0:1], wqk, wvt)
    km = km.reshape(t // MOBA_BLOCK, d)
    attn = _attn_call(q, k, vt, km, batch, seq)
    x1 = _ffn_call(xf, attn, attn_w_o[0].astype(_BF16), ffn_norm[0:1],
                   ffn_w_in[0].astype(_BF16), ffn_w_out[0].astype(_BF16), final_norm[None, :],
                   final_norm=False)

    gated = _conv_call(x1, conv_norm[0:1], conv_w_in[0].astype(_BF16), conv_w[0], seq)
    out = _ffn_call(x1, gated, conv_w_out[0].astype(_BF16), ffn_norm[1:2],
                    ffn_w_in[1].astype(_BF16), ffn_w_out[1].astype(_BF16), final_norm[None, :],
                    final_norm=True)
    return out.reshape(batch, seq, d)
```
